```python
import jax
import jax.numpy as jnp
from jax import lax
import numpy as np

D_MODEL = 2048
BATCH = 8
SEQ = 2048
DEPTH = 2

GRID_W = 64
CTX_LEN = 256
HEAD_DIM = 128
N_HEADS = 8
N_KV_HEADS = 2
KV_GROUP = N_HEADS // N_KV_HEADS
Q_W = N_HEADS * HEAD_DIM
KV_W = N_KV_HEADS * HEAD_DIM
Q_BLOCK = 128
ROPE_THETA = 10000.0
ROPE_AXIS_DIM = HEAD_DIM // 2
POOL_WINDOWS = (2, 4, 8, 16)
POOL_GROUPS = len(POOL_WINDOWS)
POOL_W = D_MODEL // 4
POOL_GC = POOL_W // POOL_GROUPS
SGU_W = D_MODEL // 4
SGU_GROUPS = 4
SGU_GC = SGU_W // SGU_GROUPS
SGU_CHUNK = 128
CONV_W = D_MODEL // 4
CONV_K = 3
N_BRANCH = 4
D_FF = -(-8 * D_MODEL // (3 * 256)) * 256
ALPHA = (2 * DEPTH) ** 0.25
BETA = (8 * DEPTH) ** -0.25
LN_EPS = 1e-5
RMS_EPS = 1e-6

OFF_K = Q_W
OFF_V = OFF_K + KV_W
OFF_POOL = OFF_V + KV_W
OFF_U = OFF_POOL + POOL_W
OFF_VG = OFF_U + SGU_W
OFF_CB = OFF_VG + SGU_W
OFF_CC = OFF_CB + CONV_W
OFF_CX = OFF_CC + CONV_W
IN_W = OFF_CX + CONV_W

kernel_name = 'hybrid_diffusion_parallel_mixer'


def layer_norm(x, g, b):
    xf = x.astype(jnp.float32)
    mu = jnp.mean(xf, axis=-1, keepdims=True)
    var = jnp.mean(jnp.square(xf - mu), axis=-1, keepdims=True)
    return ((xf - mu) * lax.rsqrt(var + LN_EPS) * g + b).astype(x.dtype)


def rms_norm(x, g):
    xf = x.astype(jnp.float32)
    ms = jnp.mean(jnp.square(xf), axis=-1, keepdims=True)
    return (xf * lax.rsqrt(ms + RMS_EPS) * g).astype(x.dtype)


def axial_rope_tables(n, dtype):
    rows = n // GRID_W
    row = jnp.repeat(jnp.arange(rows), GRID_W).astype(jnp.float32)
    col = jnp.tile(jnp.arange(GRID_W), rows).astype(jnp.float32)
    inv = ROPE_THETA ** (-jnp.arange(0, ROPE_AXIS_DIM, 2, dtype=jnp.float32) / ROPE_AXIS_DIM)
    ang_r = row[:, None] * inv
    ang_c = col[:, None] * inv
    f = lambda a: a[None, :, None, :].astype(dtype)
    return (f(jnp.cos(ang_r)), f(jnp.sin(ang_r)), f(jnp.cos(ang_c)), f(jnp.sin(ang_c)))


def rotate(x, cos, sin):
    h = x.shape[-1] // 2
    x1, x2 = x[..., :h], x[..., h:]
    return jnp.concatenate([x1 * cos - x2 * sin, x2 * cos + x1 * sin], axis=-1)


def apply_axial_rope(x, tables):
    cr, sr, cc, sc = tables
    return jnp.concatenate([rotate(x[..., :ROPE_AXIS_DIM], cr, sr),
                            rotate(x[..., ROPE_AXIS_DIM:], cc, sc)], axis=-1)


def block_attention(q, k, v):
    b, n, h, d = q.shape
    nb = n // Q_BLOCK
    qb = q.reshape(b, nb, Q_BLOCK, N_KV_HEADS, KV_GROUP, d).transpose(1, 0, 2, 3, 4, 5)
    scale = d ** -0.5

    def one_block(qi):
        s = jnp.einsum('bqkgd,bskd->bkgqs', qi, k, preferred_element_type=jnp.float32) * scale
        p = jax.nn.softmax(s, axis=-1).astype(v.dtype)
        return jnp.einsum('bkgqs,bskd->bqkgd', p, v)

    o = lax.map(one_block, qb)
    return o.transpose(1, 0, 2, 3, 4, 5).reshape(b, n, h * d)


def kv_heads(pkv, k_g):
    b, n, _ = pkv.shape
    k = rms_norm(pkv[..., :KV_W].reshape(b, n, N_KV_HEADS, HEAD_DIM), k_g)
    v = pkv[..., KV_W:].reshape(b, n, N_KV_HEADS, HEAD_DIM)
    return k, v


def multiscale_pool(z):
    b, n, ch = z.shape
    cs = jnp.cumsum(z.astype(jnp.float32), axis=1)
    cs = jnp.concatenate([jnp.zeros((b, 1, ch), jnp.float32), cs], axis=1)
    t = jnp.arange(n)
    means = []
    for g, w in enumerate(POOL_WINDOWS):
        left = w // 2
        right = w - 1 - left
        hi = jnp.clip(t + right + 1, 0, n)
        lo = jnp.clip(t - left, 0, n)
        seg = cs[:, :, g * POOL_GC:(g + 1) * POOL_GC]
        total = jnp.take(seg, hi, axis=1) - jnp.take(seg, lo, axis=1)
        means.append(total / (hi - lo).astype(jnp.float32)[None, :, None])
    return jnp.concatenate(means, axis=-1).astype(z.dtype) - z


def spatial_gating(u, v, ln_g, ln_b, w_s, b_s):
    b, n, _ = v.shape
    v = layer_norm(v, ln_g, ln_b)
    vc = v.reshape(b, n // SGU_CHUNK, SGU_CHUNK, SGU_GROUPS, SGU_GC)
    s = jnp.einsum('gpq,bnqgc->bnpgc', w_s, vc) + b_s.T[None, None, :, :, None]
    return u * s.reshape(b, n, SGU_W)


def short_conv(z, w):
    zp = jnp.pad(z, ((0, 0), (1, 1), (0, 0)))
    return zp[:, :-2] * w[0] + zp[:, 1:-1] * w[1] + zp[:, 2:] * w[2]


def local_branches(pr, pool_w, pool_scale, sgu_ln_g, sgu_ln_b, sgu_w, sgu_b, conv_w):
    b, n, _ = pr.shape
    d = multiscale_pool(pr[..., OFF_POOL:OFF_U])
    y_pool = jnp.einsum('bngc,gce->bnge', d.reshape(b, n, POOL_GROUPS, POOL_GC), pool_w).reshape(b, n, POOL_W) * pool_scale
    y_sgu = spatial_gating(jax.nn.gelu(pr[..., OFF_U:OFF_VG]), jax.nn.gelu(pr[..., OFF_VG:OFF_CB]),
                           sgu_ln_g, sgu_ln_b, sgu_w, sgu_b)
    y_conv = pr[..., OFF_CB:OFF_CC] * short_conv(pr[..., OFF_CC:OFF_CX] * pr[..., OFF_CX:IN_W], conv_w)
    return y_pool, y_sgu, y_conv


def merge_branches(h, branches, w_brs, w_gate, b_gate, w_o):
    terms = []
    for k in range(N_BRANCH):
        g = jax.nn.sigmoid(h @ w_gate[:, k * D_MODEL:(k + 1) * D_MODEL] + b_gate[k * D_MODEL:(k + 1) * D_MODEL])
        terms.append(g * (branches[k] @ w_brs[k]))
    merged = terms[0] + terms[1] + terms[2] + terms[3]
    return merged @ w_o


def hybrid_mixer(h_c, h_l, rope, ctx_out, w_in, q_g, k_g, pool_w, pool_scale, sgu_ln_g, sgu_ln_b,
                 sgu_w, sgu_b, conv_w, w_br_attn, w_br_pool, w_br_sgu, w_br_conv, w_gate, b_gate, w_o):
    b, n, _ = h_l.shape
    nc = h_c.shape[1]
    w_brs = (w_br_attn, w_br_pool, w_br_sgu, w_br_conv)
    p_l = h_l @ w_in
    q_l = apply_axial_rope(rms_norm(p_l[..., :Q_W].reshape(b, n, N_HEADS, HEAD_DIM), q_g), rope)
    k_l, v_l = kv_heads(p_l[..., OFF_K:OFF_POOL], k_g)
    k_l = apply_axial_rope(k_l, rope)
    if ctx_out:
        p_c = h_c @ w_in
        pkv_c = p_c[..., OFF_K:OFF_POOL]
    else:
        pkv_c = h_c @ w_in[:, OFF_K:OFF_POOL]
    k_c, v_c = kv_heads(pkv_c, k_g)
    a_l = block_attention(q_l, jnp.concatenate([k_c, k_l], axis=1), jnp.concatenate([v_c, v_l], axis=1))
    out_l = merge_branches(h_l, (a_l,) + local_branches(p_l, pool_w, pool_scale, sgu_ln_g, sgu_ln_b, sgu_w, sgu_b, conv_w),
                           w_brs, w_gate, b_gate, w_o)
    if not ctx_out:
        return None, out_l
    q_c = rms_norm(p_c[..., :Q_W].reshape(b, nc, N_HEADS, HEAD_DIM), q_g)
    a_c = block_attention(q_c, k_c, v_c)
    out_c = merge_branches(h_c, (a_c,) + local_branches(p_c, pool_w, pool_scale, sgu_ln_g, sgu_ln_b, sgu_w, sgu_b, conv_w),
                           w_brs, w_gate, b_gate, w_o)
    return out_c, out_l


def swiglu(h, w_g, w_u, w_d):
    return (jax.nn.silu(h @ w_g) * (h @ w_u)) @ w_d


def setup_inputs(seed: int = 0) -> dict:
    key = jax.random.key(seed)
    ks = jax.random.split(key, 32)

    def nrm(k, shape, std):
        return jax.random.normal(k, shape, jnp.float32) * std

    def gain(k, shape):
        return 1.0 + nrm(k, shape, 0.02)

    L = DEPTH
    d = D_MODEL
    return {
        'x': nrm(ks[0], (BATCH, SEQ, d), 1.0),
        'c': nrm(ks[1], (BATCH, d), 1.0),
        'ctx': nrm(ks[2], (BATCH, CTX_LEN, d), 1.0),
        'c_ctx': nrm(ks[3], (d,), 1.0),
        'w_ada': nrm(ks[4], (L, d, 6 * d), 0.5 * d ** -0.5),
        'b_ada': nrm(ks[5], (L, 6 * d), 0.01),
        'w_in': nrm(ks[6], (L, d, IN_W), d ** -0.5),
        'q_norm_g': gain(ks[7], (L, HEAD_DIM)),
        'k_norm_g': gain(ks[8], (L, HEAD_DIM)),
        'pool_w': nrm(ks[9], (L, POOL_GROUPS, POOL_GC, POOL_GC), POOL_GC ** -0.5),
        'pool_scale': 1.0 + nrm(ks[10], (L, POOL_W), 0.1),
        'sgu_ln_g': gain(ks[11], (L, SGU_W)),
        'sgu_ln_b': nrm(ks[12], (L, SGU_W), 0.02),
        'sgu_w': nrm(ks[13], (L, SGU_GROUPS, SGU_CHUNK, SGU_CHUNK), SGU_CHUNK ** -0.5),
        'sgu_b': 1.0 + nrm(ks[14], (L, SGU_GROUPS, SGU_CHUNK), 0.02),
        'conv_w': nrm(ks[15], (L, CONV_K, CONV_W), CONV_K ** -0.5),
        'w_br_attn': nrm(ks[16], (L, Q_W, d), Q_W ** -0.5),
        'w_br_pool': nrm(ks[17], (L, POOL_W, d), POOL_W ** -0.5),
        'w_br_sgu': nrm(ks[18], (L, SGU_W, d), SGU_W ** -0.5),
        'w_br_conv': nrm(ks[19], (L, CONV_W, d), CONV_W ** -0.5),
        'w_gate': nrm(ks[20], (L, d, N_BRANCH * d), d ** -0.5),
        'b_gate': nrm(ks[21], (L, N_BRANCH * d), 0.01),
        'w_o': nrm(ks[22], (L, d, d), BETA * d ** -0.5),
        'ln1_g': gain(ks[23], (L, d)),
        'ln1_b': nrm(ks[24], (L, d), 0.02),
        'w_ff_gate': nrm(ks[25], (L, d, D_FF), d ** -0.5),
        'w_ff_up': nrm(ks[26], (L, d, D_FF), d ** -0.5),
        'w_ff_down': nrm(ks[27], (L, D_FF, d), BETA * D_FF ** -0.5),
        'ln2_g': gain(ks[28], (L, d)),
        'ln2_b': nrm(ks[29], (L, d), 0.02),
    }


def reference(x, c, ctx, c_ctx, w_ada, b_ada, w_in, q_norm_g, k_norm_g, pool_w, pool_scale,
              sgu_ln_g, sgu_ln_b, sgu_w, sgu_b, conv_w, w_br_attn, w_br_pool, w_br_sgu, w_br_conv,
              w_gate, b_gate, w_o, ln1_g, ln1_b, w_ff_gate, w_ff_up, w_ff_down, ln2_g, ln2_b):
    rope = axial_rope_tables(x.shape[1], x.dtype)
    x_l, x_c = x, ctx
    for i in range(DEPTH):
        last = i == DEPTH - 1
        mod_l = jax.nn.silu(c) @ w_ada[i] + b_ada[i]
        mod_c = jax.nn.silu(c_ctx) @ w_ada[i] + b_ada[i]
        sh1_l, sc1_l, g1_l, sh2_l, sc2_l, g2_l = jnp.split(mod_l[:, None, :], 6, axis=-1)
        sh1_c, sc1_c, g1_c, sh2_c, sc2_c, g2_c = jnp.split(mod_c, 6, axis=-1)
        h_l = x_l * (1.0 + sc1_l) + sh1_l
        h_c = x_c * (1.0 + sc1_c) + sh1_c
        m_c, m_l = hybrid_mixer(h_c, h_l, rope, not last, w_in[i], q_norm_g[i], k_norm_g[i], pool_w[i],
                                pool_scale[i], sgu_ln_g[i], sgu_ln_b[i], sgu_w[i], sgu_b[i], conv_w[i],
                                w_br_attn[i], w_br_pool[i], w_br_sgu[i], w_br_conv[i], w_gate[i], b_gate[i], w_o[i])
        x_l = layer_norm(ALPHA * x_l + g1_l * m_l, ln1_g[i], ln1_b[i])
        h_l = x_l * (1.0 + sc2_l) + sh2_l
        x_l = layer_norm(ALPHA * x_l + g2_l * swiglu(h_l, w_ff_gate[i], w_ff_up[i], w_ff_down[i]), ln2_g[i], ln2_b[i])
        if not last:
            x_c = layer_norm(ALPHA * x_c + g1_c * m_c, ln1_g[i], ln1_b[i])
            h_c = x_c * (1.0 + sc2_c) + sh2_c
            x_c = layer_norm(ALPHA * x_c + g2_c * swiglu(h_c, w_ff_gate[i], w_ff_up[i], w_ff_down[i]), ln2_g[i], ln2_b[i])
    return x_l
```

```python
import functools

import numpy as np
import jax
import jax.numpy as jnp
from jax import lax
from jax.experimental import pallas as pl
from jax.experimental.pallas import tpu as pltpu

D_MODEL = 2048
GRID_W = 64
HEAD_DIM = 128
N_HEADS = 8
N_KV_HEADS = 2
KV_GROUP = N_HEADS // N_KV_HEADS
Q_W = N_HEADS * HEAD_DIM
KV_W = N_KV_HEADS * HEAD_DIM
ROPE_THETA = 10000.0
ROPE_AXIS_DIM = HEAD_DIM // 2
POOL_WINDOWS = (2, 4, 8, 16)
POOL_GC = 128
SGU_GROUPS = 4
SGU_GC = 128
SGU_CHUNK = 128
BR_W = D_MODEL // 4
N_BRANCH = 4
LOCAL_W = 6 * BR_W
OFF_K = Q_W
OFF_LOCAL = Q_W + 2 * KV_W
LN_EPS = 1e-5
RMS_EPS = 1e-6

V7X_VMEM_BYTES = 64 * 1024 * 1024
SUBLANES = 8
LANES = 128
POOL_PAD = SUBLANES

SH1, SC1, G1, SH2, SC2, G2 = range(6)
MOD_ROWS = 16

F32 = jnp.float32
BF16 = jnp.bfloat16


def _vmem_limit(nbytes):
    return int(min(V7X_VMEM_BYTES - 4 * 1024 * 1024, nbytes))


def _dot(a, b):
    return jnp.dot(a, b, preferred_element_type=F32)


def _dot_nt(a, b):
    return lax.dot_general(a, b, (((1,), (1,)), ((), ())), preferred_element_type=F32)


def _sigmoid(x):
    return 1.0 / (1.0 + jnp.exp(-x))


def _gelu_tanh(x):
    c = np.float32(np.sqrt(2.0 / np.pi))
    return x * (0.5 * (1.0 + jnp.tanh(c * (x + np.float32(0.044715) * (x * x * x)))))


def _layer_norm(y, g, b):
    mu = jnp.mean(y, axis=-1, keepdims=True)
    yc = y - mu
    var = jnp.mean(yc * yc, axis=-1, keepdims=True)
    return yc * lax.rsqrt(var + LN_EPS) * g + b


def _ada_kernel(c_ref, w_ref, b_ref, o_ref):
    c = c_ref[...]
    a = (c * _sigmoid(c)).astype(BF16)
    o_ref[0] = _dot(a, w_ref[0].astype(BF16)) + b_ref[0]


def _ada(c_all, w_ada, b_ada):
    n_layers, d, n_out = w_ada.shape
    tn = 1024
    return pl.pallas_call(
        _ada_kernel,
        out_shape=jax.ShapeDtypeStruct((n_layers, MOD_ROWS, n_out), F32),
        grid=(n_layers, n_out // tn),
        in_specs=[
            pl.BlockSpec((MOD_ROWS, d), lambda l, j: (0, 0)),
            pl.BlockSpec((1, d, tn), lambda l, j: (l, 0, j)),
            pl.BlockSpec((1, 1, tn), lambda l, j: (l, 0, j)),
        ],
        out_specs=pl.BlockSpec((1, MOD_ROWS, tn), lambda l, j: (l, 0, j)),
        compiler_params=pltpu.CompilerParams(
            dimension_semantics=("arbitrary", "arbitrary"),
            vmem_limit_bytes=_vmem_limit(2 * d * tn * 4 + d * tn * 2 + 8 * 1024 * 1024)),
        name="ada",
    )(c_all, w_ada, b_ada.reshape(n_layers, 1, n_out))


def _norm_rope(x, g, cos, sin_lo, sin_hi):
    ms = jnp.mean(x * x, axis=-1, keepdims=True)
    xn = x * lax.rsqrt(ms + RMS_EPS) * g
    return (xn * cos + pltpu.roll(xn, LANES - 32, 1) * sin_lo + pltpu.roll(xn, 32, 1) * sin_hi)


def _inproj_kernel(x_ref, mod_ref, w_ref, qg_ref, kg_ref, rope_ref, h_ref, q_ref, k_ref, v_ref, p_ref):
    x = x_ref[...]
    h = (x * (1.0 + mod_ref[0, SC1:SC1 + 1, :]) + mod_ref[0, SH1:SH1 + 1, :]).astype(BF16)
    h_ref[...] = h
    seg = 4 * HEAD_DIM
    for s in range(Q_W // seg):
        acc = _dot(h, w_ref[:, s * seg:(s + 1) * seg])
        for i in range(4):
            head = acc[:, i * HEAD_DIM:(i + 1) * HEAD_DIM]
            q_ref[:, s * seg + i * HEAD_DIM:s * seg + (i + 1) * HEAD_DIM] = _norm_rope(
                head, qg_ref[...], rope_ref[0], rope_ref[1], rope_ref[2]).astype(BF16)
    acc = _dot(h, w_ref[:, OFF_K:OFF_LOCAL])
    for i in range(N_KV_HEADS):
        head = acc[:, i * HEAD_DIM:(i + 1) * HEAD_DIM]
        k_ref[:, i * HEAD_DIM:(i + 1) * HEAD_DIM] = _norm_rope(
            head, kg_ref[...], rope_ref[3], rope_ref[4], rope_ref[5]).astype(BF16)
    v_ref[...] = acc[:, KV_W:2 * KV_W].astype(BF16)
    for s in range(LOCAL_W // BR_W):
        p_ref[:, s * BR_W:(s + 1) * BR_W] = _dot(h, w_ref[:, OFF_LOCAL + s * BR_W:OFF_LOCAL + (s + 1) * BR_W])


def _kv_kernel(x_ref, mod_ref, w_ref, kg_ref, k_ref, v_ref):
    x = x_ref[...]
    h = (x * (1.0 + mod_ref[0, SC1:SC1 + 1, :]) + mod_ref[0, SH1:SH1 + 1, :]).astype(BF16)
    acc = _dot(h, w_ref[...])
    for i in range(N_KV_HEADS):
        head = acc[:, i * HEAD_DIM:(i + 1) * HEAD_DIM]
        ms = jnp.mean(head * head, axis=-1, keepdims=True)
        k_ref[:, i * HEAD_DIM:(i + 1) * HEAD_DIM] = (head * lax.rsqrt(ms + RMS_EPS) * kg_ref[...]).astype(BF16)
    v_ref[...] = acc[:, KV_W:2 * KV_W].astype(BF16)


def _mod_spec(mod_row):
    return pl.BlockSpec((1, 6, D_MODEL), lambda i, *_: (mod_row(i), 0, 0))


def _inproj(x, mod, mod_row, w_in, qg, kg, rope, seq_len, tm):
    m, d = x.shape
    in_w = w_in.shape[1]
    steps_per_seq = seq_len // tm
    row = lambda i: (i, 0)
    vmem = (d * in_w * 2 + 2 * tm * (d * 4 + d * 2 + Q_W * 2 + 2 * KV_W * 2 + LOCAL_W * 4 + 6 * LANES * 4)
            + 3 * tm * BR_W * 4 + tm * d * 2 + 4 * 1024 * 1024)
    return pl.pallas_call(
        _inproj_kernel,
        out_shape=(
            jax.ShapeDtypeStruct((m, d), BF16),
            jax.ShapeDtypeStruct((m, Q_W), BF16),
            jax.ShapeDtypeStruct((m, KV_W), BF16),
            jax.ShapeDtypeStruct((m, KV_W), BF16),
            jax.ShapeDtypeStruct((m, LOCAL_W), F32),
        ),
        grid=(m // tm,),
        in_specs=[
            pl.BlockSpec((tm, d), row),
            _mod_spec(mod_row),
            pl.BlockSpec((d, in_w), lambda i: (0, 0), pipeline_mode=pl.Buffered(1)),
            pl.BlockSpec((1, HEAD_DIM), lambda i: (0, 0)),
            pl.BlockSpec((1, HEAD_DIM), lambda i: (0, 0)),
            pl.BlockSpec((6, tm, LANES), lambda i: (0, i % steps_per_seq, 0)),
        ],
        out_specs=(
            pl.BlockSpec((tm, d), row),
            pl.BlockSpec((tm, Q_W), row),
            pl.BlockSpec((tm, KV_W), row),
            pl.BlockSpec((tm, KV_W), row),
            pl.BlockSpec((tm, LOCAL_W), row),
        ),
        compiler_params=pltpu.CompilerParams(
            dimension_semantics=("parallel",), vmem_limit_bytes=_vmem_limit(vmem)),
        name="inproj",
    )(x, mod, w_in, qg, kg, rope)


def _kv_proj(x, mod, mod_row, w_kv, kg, tm):
    m, d = x.shape
    row = lambda i: (i, 0)
    return pl.pallas_call(
        _kv_kernel,
        out_shape=(jax.ShapeDtypeStruct((m, KV_W), BF16), jax.ShapeDtypeStruct((m, KV_W), BF16)),
        grid=(m // tm,),
        in_specs=[
            pl.BlockSpec((tm, d), row),
            _mod_spec(mod_row),
            pl.BlockSpec((d, 2 * KV_W), lambda i: (0, 0)),
            pl.BlockSpec((1, HEAD_DIM), lambda i: (0, 0)),
        ],
        out_specs=(pl.BlockSpec((tm, KV_W), row), pl.BlockSpec((tm, KV_W), row)),
        compiler_params=pltpu.CompilerParams(dimension_semantics=("parallel",)),
        name="kv_proj",
    )(x, mod, w_kv, kg)


def _pool_kernel(z_ref, w_ref, sc_ref, o_ref, *, n):
    t = lax.broadcasted_iota(jnp.int32, (n, POOL_GC), 0)
    zeros = jnp.zeros((POOL_PAD, POOL_GC), F32)
    for g, w in enumerate(POOL_WINDOWS):
        left = w // 2
        right = w - 1 - left
        z = z_ref[0, :, g * POOL_GC:(g + 1) * POOL_GC]
        tsum = jnp.concatenate([zeros, z, zeros], axis=0)
        k = 1
        while k < w:
            tsum = tsum + pltpu.roll(tsum, k, 0)
            k *= 2
        if right:
            tsum = pltpu.roll(tsum, n + 2 * POOL_PAD - right, 0)
        total = tsum[POOL_PAD:POOL_PAD + n]
        cnt = (jnp.minimum(t + (right + 1), n) - jnp.maximum(t - left, 0)).astype(F32)
        dlt = (total / cnt - z).astype(BF16)
        y = _dot(dlt, w_ref[g]) * sc_ref[:, g * POOL_GC:(g + 1) * POOL_GC]
        o_ref[0, :, g * POOL_GC:(g + 1) * POOL_GC] = y.astype(BF16)


def _sgu_kernel(u_ref, v_ref, g_ref, b_ref, w_ref, bs_ref, o_ref, vn_scr, *, n):
    v = _gelu_tanh(v_ref[0])
    vn_scr[...] = _layer_norm(v, g_ref[...], b_ref[...]).astype(BF16)

    def chunk(c, carry):
        r0 = pl.multiple_of(c * SGU_CHUNK, SGU_CHUNK)
        for g in range(SGU_GROUPS):
            cols = slice(g * SGU_GC, (g + 1) * SGU_GC)
            s = _dot(w_ref[g], vn_scr[pl.ds(r0, SGU_CHUNK), cols]) + bs_ref[g]
            u = _gelu_tanh(u_ref[0, pl.ds(r0, SGU_CHUNK), cols])
            o_ref[0, pl.ds(r0, SGU_CHUNK), cols] = (u * s).astype(BF16)
        return carry

    lax.fori_loop(0, n // SGU_CHUNK, chunk, 0)


def _conv_kernel(cb_ref, cc_ref, cx_ref, w_ref, o_ref, *, n):
    t = lax.broadcasted_iota(jnp.int32, (n, LANES), 0)
    for g in range(BR_W // LANES):
        cols = slice(g * LANES, (g + 1) * LANES)
        a = cc_ref[0, :, cols] * cx_ref[0, :, cols]
        prev = jnp.where(t == 0, 0.0, pltpu.roll(a, 1, 0))
        nxt = jnp.where(t == n - 1, 0.0, pltpu.roll(a, n - 1, 0))
        y = prev * w_ref[0:1, cols] + a * w_ref[1:2, cols] + nxt * w_ref[2:3, cols]
        o_ref[0, :, cols] = (cb_ref[0, :, cols] * y).astype(BF16)


def _local_specs(n, first, count):
    return [pl.BlockSpec((1, n, BR_W), (lambda b, s=first + i: (b, 0, s))) for i in range(count)]


def _local_branches(p_loc, n, pool_w, pool_scale, sgu_g, sgu_b, sgu_w, sgu_bs, conv_w):
    n_seq = p_loc.shape[0]
    out = jax.ShapeDtypeStruct((n_seq, n, BR_W), BF16)
    out_spec = pl.BlockSpec((1, n, BR_W), lambda b: (b, 0, 0))
    blk = n * BR_W * 4
    whole = lambda *shape: pl.BlockSpec(shape, lambda b: (0,) * len(shape))
    params = lambda n_in: pltpu.CompilerParams(
        dimension_semantics=("parallel",), vmem_limit_bytes=_vmem_limit((2 * n_in + 7) * blk + 4 * 1024 * 1024))
    y_pool = pl.pallas_call(
        functools.partial(_pool_kernel, n=n), out_shape=out, grid=(n_seq,),
        in_specs=_local_specs(n, 0, 1) + [whole(4, POOL_GC, POOL_GC), whole(1, BR_W)],
        out_specs=out_spec, compiler_params=params(1), name="pool",
    )(p_loc, pool_w, pool_scale)
    y_sgu = pl.pallas_call(
        functools.partial(_sgu_kernel, n=n), out_shape=out, grid=(n_seq,),
        in_specs=_local_specs(n, 1, 2) + [whole(1, BR_W), whole(1, BR_W), whole(4, SGU_CHUNK, SGU_CHUNK),
                                          whole(4, SGU_CHUNK, SGU_GC)],
        out_specs=out_spec, scratch_shapes=[pltpu.VMEM((n, BR_W), BF16)],
        compiler_params=params(2), name="sgu",
    )(p_loc, p_loc, sgu_g, sgu_b, sgu_w, sgu_bs)
    y_conv = pl.pallas_call(
        functools.partial(_conv_kernel, n=n), out_shape=out, grid=(n_seq,),
        in_specs=_local_specs(n, 3, 3) + [whole(3, BR_W)],
        out_specs=out_spec, compiler_params=params(3), name="conv",
    )(p_loc, p_loc, p_loc, conv_w)
    return y_pool, y_sgu, y_conv


def _attn_kernel(q_ref, *refs, n_kv):
    k_refs, v_refs, o_ref = refs[:n_kv], refs[n_kv:2 * n_kv], refs[2 * n_kv]
    ks = [r[0] for r in k_refs]
    vs = [r[0] for r in v_refs]
    for g in range(KV_GROUP):
        cols = slice(g * HEAD_DIM, (g + 1) * HEAD_DIM)
        q = q_ref[0, :, cols]
        ss = [_dot_nt(q, k) for k in ks]
        m = functools.reduce(jnp.maximum, [jnp.max(s, axis=-1, keepdims=True) for s in ss])
        ps = [jnp.exp(s - m) for s in ss]
        denom = functools.reduce(jnp.add, [jnp.sum(p, axis=-1, keepdims=True) for p in ps])
        o = functools.reduce(jnp.add, [_dot(p.astype(BF16), v) for p, v in zip(ps, vs)])
        o_ref[0, :, cols] = (o / denom).astype(BF16)


def _attention(q, ks, vs, tq):
    b, n, _ = q.shape
    gw = KV_GROUP * HEAD_DIM
    kv_specs = [pl.BlockSpec((1, k.shape[1], HEAD_DIM), lambda bi, kh, qi: (bi, 0, kh)) for k in ks]
    s_total = sum(k.shape[1] for k in ks)
    vmem = 4 * s_total * HEAD_DIM * 2 * 2 + 4 * tq * gw * 2 + 6 * tq * s_total * 4 + 4 * 1024 * 1024
    return pl.pallas_call(
        functools.partial(_attn_kernel, n_kv=len(ks)),
        out_shape=jax.ShapeDtypeStruct((b, n, Q_W), BF16),
        grid=(b, N_KV_HEADS, n // tq),
        in_specs=[pl.BlockSpec((1, tq, gw), lambda bi, kh, qi: (bi, qi, kh))] + kv_specs + kv_specs,
        out_specs=pl.BlockSpec((1, tq, gw), lambda bi, kh, qi: (bi, qi, kh)),
        compiler_params=pltpu.CompilerParams(
            dimension_semantics=("parallel", "parallel", "arbitrary"), vmem_limit_bytes=_vmem_limit(vmem)),
        name="attn",
    )(q, *ks, *vs)


def _merge_kernel(h_ref, a_ref, yp_ref, ys_ref, yc_ref, wg0, wg1, wg2, wg3, bg0, bg1, bg2, bg3,
                  wa_ref, wp_ref, ws_ref, wc_ref, o_ref):
    h = h_ref[...]
    acc = None
    for br, wg, bg, wb in ((a_ref, wg0, bg0, wa_ref), (yp_ref, wg1, bg1, wp_ref),
                           (ys_ref, wg2, bg2, ws_ref), (yc_ref, wg3, bg3, wc_ref)):
        gate = _sigmoid(_dot(h, wg[...]) + bg[...])
        term = gate * _dot(br[...], wb[...])
        acc = term if acc is None else acc + term
    o_ref[...] = acc.astype(BF16)


def _merge(h, a, yp, ys, yc, w_gate, b_gate, w_brs, tm, tn):
    m, d = h.shape
    nt = d // tn
    row = lambda w: pl.BlockSpec((tm, w), lambda i, j: (i, 0))
    wg_specs = [pl.BlockSpec((d, tn), (lambda i, j, k=k: (0, k * nt + j))) for k in range(N_BRANCH)]
    bg_specs = [pl.BlockSpec((1, tn), (lambda i, j, k=k: (0, k * nt + j))) for k in range(N_BRANCH)]
    wb_specs = [pl.BlockSpec((w.shape[0], tn), lambda i, j: (0, j)) for w in w_brs]
    k_br = sum(w.shape[0] for w in w_brs)
    vmem = 2 * (tm * (d + k_br) * 2 + (4 * d + k_br) * tn * 2 + tm * tn * 2) + 6 * tm * tn * 4 + 4 * 1024 * 1024
    return pl.pallas_call(
        _merge_kernel,
        out_shape=jax.ShapeDtypeStruct((m, d), BF16),
        grid=(m // tm, nt),
        in_specs=[row(d), row(Q_W), row(BR_W), row(BR_W), row(BR_W)] + wg_specs + bg_specs + wb_specs,
        out_specs=pl.BlockSpec((tm, tn), lambda i, j: (i, j)),
        compiler_params=pltpu.CompilerParams(
            dimension_semantics=("parallel", "arbitrary"), vmem_limit_bytes=_vmem_limit(vmem)),
        name="merge",
    )(h, a, yp, ys, yc, w_gate, w_gate, w_gate, w_gate, b_gate, b_gate, b_gate, b_gate, *w_brs)


def _oproj_kernel(m_ref, x_ref, mod_ref, w_ref, g_ref, b_ref, o_ref, *, alpha):
    out = _dot(m_ref[...], w_ref[...])
    y = alpha * x_ref[...] + mod_ref[0, G1:G1 + 1, :] * out
    o_ref[...] = _layer_norm(y, g_ref[...], b_ref[...])


def _oproj(merged, x, mod, mod_row, w_o, ln_g, ln_b, alpha, tm):
    m, d = x.shape
    row = lambda i: (i, 0)
    vec = pl.BlockSpec((1, d), lambda i: (0, 0))
    vmem = d * d * 2 + 2 * tm * d * (2 + 4 + 4) + 3 * tm * d * 4 + 4 * 1024 * 1024
    return pl.pallas_call(
        functools.partial(_oproj_kernel, alpha=alpha),
        out_shape=jax.ShapeDtypeStruct((m, d), F32),
        grid=(m // tm,),
        in_specs=[pl.BlockSpec((tm, d), row), pl.BlockSpec((tm, d), row), _mod_spec(mod_row),
                  pl.BlockSpec((d, d), lambda i: (0, 0), pipeline_mode=pl.Buffered(1)), vec, vec],
        out_specs=pl.BlockSpec((tm, d), row),
        compiler_params=pltpu.CompilerParams(
            dimension_semantics=("parallel",), vmem_limit_bytes=_vmem_limit(vmem)),
        name="oproj",
    )(merged, x, mod, w_o, ln_g, ln_b)


def _ffn_kernel(x_ref, mod_ref, wg_ref, wu_ref, wd_ref, g_ref, b_ref, o_ref, h_scr, *, alpha, nc):
    j = pl.program_id(1)

    @pl.when(j == 0)
    def _():
        x = x_ref[...]
        h_scr[...] = (x * (1.0 + mod_ref[0, SC2:SC2 + 1, :]) + mod_ref[0, SH2:SH2 + 1, :]).astype(BF16)
        o_ref[...] = alpha * x

    h = h_scr[...]
    gate = _dot(h, wg_ref[...])
    act = (gate * _sigmoid(gate) * _dot(h, wu_ref[...])).astype(BF16)
    cw = o_ref.shape[1] // nc
    for c in range(nc):
        cols = slice(c * cw, (c + 1) * cw)
        o_ref[:, cols] += mod_ref[0, G2:G2 + 1, cols] * _dot(act, wd_ref[:, cols])

    @pl.when(j == pl.num_programs(1) - 1)
    def _():
        o_ref[...] = _layer_norm(o_ref[...], g_ref[...], b_ref[...])


def _ffn(x, mod, mod_row, w_g, w_u, w_d, ln_g, ln_b, alpha, tm, tf):
    m, d = x.shape
    f = w_g.shape[1]
    vec = pl.BlockSpec((1, d), lambda i, j: (0, 0))
    vmem = 3 * tm * d * 4 + 6 * d * tf * 2 + tm * d * 2 + 4 * tm * tf * 4 + tm * d * 4 + 4 * 1024 * 1024
    return pl.pallas_call(
        functools.partial(_ffn_kernel, alpha=alpha, nc=4),
        out_shape=jax.ShapeDtypeStruct((m, d), F32),
        grid=(m // tm, f // tf),
        in_specs=[
            pl.BlockSpec((tm, d), lambda i, j: (i, 0), pipeline_mode=pl.Buffered(1)),
            _mod_spec(mod_row),
            pl.BlockSpec((d, tf), lambda i, j: (0, j)),
            pl.BlockSpec((d, tf), lambda i, j: (0, j)),
            pl.BlockSpec((tf, d), lambda i, j: (j, 0)),
            vec, vec,
        ],
        out_specs=pl.BlockSpec((tm, d), lambda i, j: (i, 0)),
        scratch_shapes=[pltpu.VMEM((tm, d), BF16)],
        compiler_params=pltpu.CompilerParams(
            dimension_semantics=("parallel", "arbitrary"), vmem_limit_bytes=_vmem_limit(vmem)),
        name="ffn",
    )(x, mod, w_g, w_u, w_d, ln_g, ln_b)


def _rope_tables(n, q_scale):
    t = np.arange(n)
    inv = ROPE_THETA ** (-np.arange(0, ROPE_AXIS_DIM, 2, dtype=np.float32) / ROPE_AXIS_DIM)
    ang_r = (t // GRID_W).astype(np.float32)[:, None] * inv
    ang_c = (t % GRID_W).astype(np.float32)[:, None] * inv
    zero = np.zeros_like(ang_r)
    cos = np.concatenate([np.cos(ang_r)] * 2 + [np.cos(ang_c)] * 2, axis=-1)
    sin_lo = np.concatenate([-np.sin(ang_r), zero, -np.sin(ang_c), zero], axis=-1)
    sin_hi = np.concatenate([zero, np.sin(ang_r), zero, np.sin(ang_c)], axis=-1)
    k_tab = np.stack([cos, sin_lo, sin_hi]).astype(np.float32)
    return jnp.asarray(np.concatenate([k_tab * np.float32(q_scale), k_tab]))


def _identity_tables(n, q_scale):
    one = np.ones((n, LANES), np.float32)
    zero = np.zeros((n, LANES), np.float32)
    return jnp.asarray(np.stack([one * np.float32(q_scale), zero, zero, one, zero, zero]))


def kernel(x, c, ctx, c_ctx, w_ada, b_ada, w_in, q_norm_g, k_norm_g, pool_w, pool_scale, sgu_ln_g, sgu_ln_b, sgu_w, sgu_b, conv_w, w_br_attn, w_br_pool, w_br_sgu, w_br_conv, w_gate, b_gate, w_o, ln1_g, ln1_b, w_ff_gate, w_ff_up, w_ff_down, ln2_g, ln2_b):
    batch, seq, d = x.shape
    n_ctx = ctx.shape[1]
    depth = w_in.shape[0]
    alpha = float((2 * depth) ** 0.25)
    q_scale = HEAD_DIM ** -0.5

    c_all = jnp.zeros((MOD_ROWS, d), F32).at[:batch].set(c).at[batch].set(c_ctx)
    mod_all = _ada(c_all, w_ada, b_ada).reshape(depth, MOD_ROWS, 6, d)

    rope_l = _rope_tables(seq, q_scale)
    rope_c = _identity_tables(n_ctx, q_scale)

    x_l = x.reshape(batch * seq, d)
    x_c = ctx.reshape(batch * n_ctx, d)
    tm_l, tm_c = 512, 256
    lat_row = lambda i, per=seq // tm_l: i // per
    ctx_row = lambda i: batch
    tm_big = 1024
    lat_row_big = lambda i, per=seq // tm_big: i // per

    for i in range(depth):
        last = i == depth - 1
        mod = mod_all[i]
        w_in_i = w_in[i].astype(BF16)
        qg = q_norm_g[i].reshape(1, HEAD_DIM)
        kg = k_norm_g[i].reshape(1, HEAD_DIM)
        local_w = (pool_w[i].astype(BF16), pool_scale[i].reshape(1, BR_W), sgu_ln_g[i].reshape(1, BR_W),
                   sgu_ln_b[i].reshape(1, BR_W), sgu_w[i].astype(BF16),
                   jnp.broadcast_to(sgu_b[i][:, :, None], (SGU_GROUPS, SGU_CHUNK, SGU_GC)), conv_w[i])
        w_brs = tuple(w[i].astype(BF16) for w in (w_br_attn, w_br_pool, w_br_sgu, w_br_conv))
        w_gate_i = w_gate[i].astype(BF16)
        b_gate_i = b_gate[i].reshape(1, N_BRANCH * d)
        w_o_i = w_o[i].astype(BF16)
        ln1 = (ln1_g[i].reshape(1, d), ln1_b[i].reshape(1, d))
        ln2 = (ln2_g[i].reshape(1, d), ln2_b[i].reshape(1, d))
        ffn_w = (w_ff_gate[i].astype(BF16), w_ff_up[i].astype(BF16), w_ff_down[i].astype(BF16))

        h_l, q_l, k_l, v_l, p_l = _inproj(x_l, mod, lat_row, w_in_i, qg, kg, rope_l, seq, tm_l)
        if last:
            k_c, v_c = _kv_proj(x_c, mod, ctx_row, w_in_i[:, OFF_K:OFF_LOCAL], kg, tm_c)
        else:
            h_c, q_c, k_c, v_c, p_c = _inproj(x_c, mod, ctx_row, w_in_i, qg, kg, rope_c, n_ctx, tm_c)
        k_c3, v_c3 = k_c.reshape(batch, n_ctx, KV_W), v_c.reshape(batch, n_ctx, KV_W)

        a_l = _attention(q_l.reshape(batch, seq, Q_W), [k_c3, k_l.reshape(batch, seq, KV_W)],
                         [v_c3, v_l.reshape(batch, seq, KV_W)], tq=256)
        y_l = _local_branches(p_l.reshape(batch, seq, LOCAL_W), seq, *local_w)
        merged = _merge(h_l, a_l.reshape(batch * seq, Q_W), *(y.reshape(batch * seq, BR_W) for y in y_l),
                        w_gate_i, b_gate_i, w_brs, tm=tm_big, tn=512)
        x_l = _oproj(merged, x_l, mod, lat_row, w_o_i, *ln1, alpha, tm_l)
        x_l = _ffn(x_l, mod, lat_row_big, *ffn_w, *ln2, alpha, tm=tm_big, tf=512)

        if not last:
            a_c = _attention(q_c.reshape(batch, n_ctx, Q_W), [k_c3], [v_c3], tq=n_ctx)
            y_c = _local_branches(p_c.reshape(batch, n_ctx, LOCAL_W), n_ctx, *local_w)
            merged_c = _merge(h_c, a_c.reshape(batch * n_ctx, Q_W), *(y.reshape(batch * n_ctx, BR_W) for y in y_c),
                              w_gate_i, b_gate_i, w_brs, tm=tm_big, tn=512)
            x_c = _oproj(merged_c, x_c, mod, ctx_row, w_o_i, *ln1, alpha, tm_l)
            x_c = _ffn(x_c, mod, ctx_row, *ffn_w, *ln2, alpha, tm=tm_big, tf=512)

    return x_l.reshape(batch, seq, d)
```
